```python
import math
import jax, jax.numpy as jnp
from jax import lax
import numpy as np

D_MODEL = 1024
BATCH = 1
SEQ = 16384
DEPTH = 1
DEC_BATCH = 128
DEC_SEQ = 4
PAST_LEN = 16384
PAGE_SIZE = 128

ROPE_THETA = 500000.0
EPS = 1e-6
Q_BLOCK = 128
N_MEM = 256

DA_HEADS = 4
DA_HEAD_DIM = 64
DA_V_DIM = 2 * DA_HEAD_DIM
DA_WIDTH = DA_HEADS * DA_V_DIM
DA_ROT = DA_HEAD_DIM // 4

MLA_HEADS = 4
MLA_NOPE = 64
MLA_ROPE = 32
MLA_QK = MLA_NOPE + MLA_ROPE
MLA_V = 64
MLA_Q_RANK = 256
MLA_KV_RANK = 128
MLA_WIDTH = MLA_HEADS * MLA_V

MEM_HEADS = 4
MEM_HEAD_DIM = 64
MEM_WIDTH = MEM_HEADS * MEM_HEAD_DIM

MIX_WIDTH = DA_WIDTH + MLA_WIDTH + MEM_WIDTH
IN_SIZES = (DA_WIDTH, DA_WIDTH, DA_WIDTH, DA_WIDTH, MLA_Q_RANK, MLA_KV_RANK, MLA_ROPE, MLA_WIDTH, MEM_WIDTH, MEM_WIDTH)
IN_COLS = sum(IN_SIZES)
IN_SPLITS = tuple(int(c) for c in np.cumsum(IN_SIZES)[:-1])

kernel_name = 'hymba_diffattn_mla_memory_step'


def rms_norm(x, w):
    xf = x.astype(jnp.float32)
    xf = xf * lax.rsqrt(jnp.mean(xf * xf, axis=-1, keepdims=True) + EPS)
    return (xf * w.astype(jnp.float32)).astype(x.dtype)


def rope(x, pos, rot_dim):
    half = rot_dim // 2
    inv_freq = ROPE_THETA ** (-jnp.arange(half, dtype=jnp.float32) / half)
    ang = pos.astype(jnp.float32)[:, None] * inv_freq[None, :]
    shape = (pos.shape[0],) + (1,) * (x.ndim - 3) + (half,)
    cos = jnp.cos(ang).reshape(shape)
    sin = jnp.sin(ang).reshape(shape)
    xr = x[..., :rot_dim].astype(jnp.float32)
    x1, x2 = xr[..., :half], xr[..., half:]
    rot = jnp.concatenate([x1 * cos - x2 * sin, x2 * cos + x1 * sin], axis=-1).astype(x.dtype)
    return jnp.concatenate([rot, x[..., rot_dim:]], axis=-1)


def diff_attend(q, k, v, q_pos, k_pos, lam):
    s = jnp.einsum('bqhcd,bkhcd->bhcqk', q, k).astype(jnp.float32) * (DA_HEAD_DIM ** -0.5)
    s = jnp.where(k_pos[None, :] <= q_pos[:, None], s, -jnp.inf)
    p = jax.nn.softmax(s, axis=-1)
    p = p[:, :, 0] - lam * p[:, :, 1]
    return jnp.einsum('bhqk,bkhd->bqhd', p.astype(v.dtype), v)


def mla_attend(q_lat, q_pe, ckv, kpe, q_pos, k_pos):
    s = (jnp.einsum('bqhr,bkr->bhqk', q_lat, ckv) + jnp.einsum('bqhd,bkd->bhqk', q_pe, kpe)).astype(jnp.float32) * (MLA_QK ** -0.5)
    s = jnp.where(k_pos[None, :] <= q_pos[:, None], s, -jnp.inf)
    p = jax.nn.softmax(s, axis=-1)
    return jnp.einsum('bhqk,bkr->bqhr', p.astype(ckv.dtype), ckv)


def mem_attend(q, k, v):
    s = jnp.einsum('bqhd,bmhd->bhqm', q, k).astype(jnp.float32) * (MEM_HEAD_DIM ** -0.5)
    p = jax.nn.softmax(s, axis=-1)
    return jnp.einsum('bhqm,bmhd->bqhd', p.astype(v.dtype), v)


def project(h, pos, p):
    B, S = h.shape[0], h.shape[1]
    u = h @ p['w_in']
    da_q, da_k, da_v, da_g, mq, mkv, mkpe, mla_g, mem_q, mem_g = jnp.split(u, IN_SPLITS, axis=-1)
    da_q = rope(rms_norm(da_q.reshape(B, S, DA_HEADS, 2, DA_HEAD_DIM), p['da_qn_w']), pos, DA_ROT)
    da_k = rope(rms_norm(da_k.reshape(B, S, DA_HEADS, 2, DA_HEAD_DIM), p['da_kn_w']), pos, DA_ROT)
    da_v = da_v.reshape(B, S, DA_HEADS, DA_V_DIM)
    q = (rms_norm(mq, p['mla_qa_norm_w']) @ p['w_mla_uq']).reshape(B, S, MLA_HEADS, MLA_QK)
    q = rms_norm(q, p['mla_qn_w'])
    q_pe = rope(q[..., MLA_NOPE:], pos, MLA_ROPE)
    q_lat = jnp.einsum('bshn,hnr->bshr', q[..., :MLA_NOPE], p['w_mla_uk'])
    ckv = rms_norm(mkv, p['mla_kv_norm_w'])
    kpe = rope(rms_norm(mkpe, p['mla_kn_w']), pos, MLA_ROPE)
    mem_q = rms_norm(mem_q.reshape(B, S, MEM_HEADS, MEM_HEAD_DIM), p['mem_qn_w'])
    return da_q, da_k, da_v, q_lat, q_pe, ckv, kpe, mem_q, (da_g, mla_g, mem_g)


def mem_kv(mem, p):
    B, M = mem.shape[0], mem.shape[1]
    kv = rms_norm(mem, p['mem_norm_w']) @ p['w_mem_kv']
    k, v = jnp.split(kv, 2, axis=-1)
    k = rms_norm(k.reshape(B, M, MEM_HEADS, MEM_HEAD_DIM), p['mem_kn_w'])
    v = v.reshape(B, M, MEM_HEADS, MEM_HEAD_DIM)
    return k, v


def merge(o_da, o_lat, o_mem, gates, p, lam_init):
    B, S = o_da.shape[0], o_da.shape[1]
    da_g, mla_g, mem_g = gates
    o_da = rms_norm(o_da, p['da_subln_w']) * (1.0 - lam_init)
    o_mla = jnp.einsum('bshr,hrd->bshd', o_lat, p['w_mla_uv'])
    o = jnp.concatenate([
        o_da.reshape(B, S, DA_WIDTH) * jax.nn.silu(da_g),
        o_mla.reshape(B, S, MLA_WIDTH) * jax.nn.silu(mla_g),
        o_mem.reshape(B, S, MEM_WIDTH) * jax.nn.silu(mem_g)], axis=-1)
    return o @ p['w_out']


def causal_blocks(attend, qs, q_pos):
    S = q_pos.shape[0]
    nb = S // Q_BLOCK
    def to_blocks(a):
        return jnp.moveaxis(a.reshape((a.shape[0], nb, Q_BLOCK) + a.shape[2:]), 1, 0)
    def from_blocks(a):
        a = jnp.moveaxis(a, 0, 1)
        return a.reshape((a.shape[0], S) + a.shape[3:])
    outs = lax.map(lambda xs: attend(xs[0], xs[1]), (tuple(to_blocks(a) for a in qs), q_pos.reshape(nb, Q_BLOCK)))
    return tuple(from_blocks(o) for o in outs)


def setup_inputs(seed: int = 0) -> dict:
    key = jax.random.key(seed)
    ks = jax.random.split(key, 40)
    f32 = jnp.float32
    n_pages = PAST_LEN // PAGE_SIZE
    n_phys = (DEC_BATCH * n_pages * 5 + 3) // 4

    def nrm(k, shape, scale=1.0):
        return scale * jax.random.normal(k, shape, f32)

    def gain(k, shape):
        return 1.0 + 0.02 * jax.random.normal(k, shape, f32)

    page_table = jax.random.permutation(ks[10], n_phys)[: DEC_BATCH * n_pages].reshape(DEC_BATCH, n_pages).astype(jnp.int32)
    return dict(
        x_prompt=nrm(ks[0], (BATCH, SEQ, D_MODEL)),
        x_sample=nrm(ks[1], (DEC_BATCH, DEC_SEQ, D_MODEL)),
        mem_prompt=nrm(ks[2], (BATCH, N_MEM, D_MODEL)),
        cache_diff_k=nrm(ks[3], (DEPTH, n_phys, PAGE_SIZE, DA_HEADS, DA_V_DIM)),
        cache_diff_v=nrm(ks[4], (DEPTH, n_phys, PAGE_SIZE, DA_HEADS, DA_V_DIM)),
        cache_mla_ckv=nrm(ks[5], (DEPTH, n_phys, PAGE_SIZE, MLA_KV_RANK)),
        cache_mla_kpe=nrm(ks[6], (DEPTH, n_phys, PAGE_SIZE, MLA_ROPE)),
        cache_mem_k=nrm(ks[7], (DEPTH, DEC_BATCH, N_MEM, MEM_HEADS, MEM_HEAD_DIM)),
        cache_mem_v=nrm(ks[8], (DEPTH, DEC_BATCH, N_MEM, MEM_HEADS, MEM_HEAD_DIM)),
        page_table=page_table,
        norm_w=gain(ks[11], (DEPTH, D_MODEL)),
        w_in=nrm(ks[12], (DEPTH, D_MODEL, IN_COLS), D_MODEL ** -0.5),
        da_qn_w=gain(ks[13], (DEPTH, DA_HEAD_DIM)),
        da_kn_w=gain(ks[14], (DEPTH, DA_HEAD_DIM)),
        da_lq1=nrm(ks[15], (DEPTH, DA_HEAD_DIM), 0.1),
        da_lk1=nrm(ks[16], (DEPTH, DA_HEAD_DIM), 0.1),
        da_lq2=nrm(ks[17], (DEPTH, DA_HEAD_DIM), 0.1),
        da_lk2=nrm(ks[18], (DEPTH, DA_HEAD_DIM), 0.1),
        da_subln_w=gain(ks[19], (DEPTH, DA_V_DIM)),
        mla_qa_norm_w=gain(ks[20], (DEPTH, MLA_Q_RANK)),
        w_mla_uq=nrm(ks[21], (DEPTH, MLA_Q_RANK, MLA_HEADS * MLA_QK), MLA_Q_RANK ** -0.5),
        mla_qn_w=gain(ks[22], (DEPTH, MLA_QK)),
        w_mla_uk=nrm(ks[23], (DEPTH, MLA_HEADS, MLA_NOPE, MLA_KV_RANK), MLA_NOPE ** -0.5),
        mla_kv_norm_w=gain(ks[24], (DEPTH, MLA_KV_RANK)),
        mla_kn_w=gain(ks[25], (DEPTH, MLA_ROPE)),
        w_mla_uv=nrm(ks[26], (DEPTH, MLA_HEADS, MLA_KV_RANK, MLA_V), MLA_KV_RANK ** -0.5),
        mem_norm_w=gain(ks[27], (DEPTH, D_MODEL)),
        w_mem_kv=nrm(ks[28], (DEPTH, D_MODEL, 2 * MEM_WIDTH), D_MODEL ** -0.5),
        mem_qn_w=gain(ks[29], (DEPTH, MEM_HEAD_DIM)),
        mem_kn_w=gain(ks[30], (DEPTH, MEM_HEAD_DIM)),
        w_out=nrm(ks[31], (DEPTH, MIX_WIDTH, D_MODEL), MIX_WIDTH ** -0.5),
    )


def reference(x_prompt, x_sample, mem_prompt, cache_diff_k, cache_diff_v, cache_mla_ckv, cache_mla_kpe,
              cache_mem_k, cache_mem_v, page_table, norm_w, w_in, da_qn_w, da_kn_w, da_lq1, da_lk1,
              da_lq2, da_lk2, da_subln_w, mla_qa_norm_w, w_mla_uq, mla_qn_w, w_mla_uk, mla_kv_norm_w,
              mla_kn_w, w_mla_uv, mem_norm_w, w_mem_kv, mem_qn_w, mem_kn_w, w_out):
    f32 = jnp.float32
    n_past = (PAST_LEN // PAGE_SIZE) * PAGE_SIZE
    B, S = x_prompt.shape[0], x_prompt.shape[1]
    DB, DS = x_sample.shape[0], x_sample.shape[1]
    pos_p = jnp.arange(S, dtype=jnp.int32)
    pos_s = PAST_LEN + jnp.arange(DS, dtype=jnp.int32)
    kpos_s = jnp.arange(PAST_LEN + DS, dtype=jnp.int32)
    xp, xs = x_prompt, x_sample
    p_dk, p_dv, p_c, p_r, p_mk, p_mv = [], [], [], [], [], []
    s_dk, s_dv, s_c, s_r = [], [], [], []
    for l in range(DEPTH):
        p = dict(w_in=w_in[l], da_qn_w=da_qn_w[l], da_kn_w=da_kn_w[l], da_subln_w=da_subln_w[l],
                 mla_qa_norm_w=mla_qa_norm_w[l], w_mla_uq=w_mla_uq[l], mla_qn_w=mla_qn_w[l],
                 w_mla_uk=w_mla_uk[l], mla_kv_norm_w=mla_kv_norm_w[l], mla_kn_w=mla_kn_w[l],
                 w_mla_uv=w_mla_uv[l], mem_norm_w=mem_norm_w[l], w_mem_kv=w_mem_kv[l],
                 mem_qn_w=mem_qn_w[l], mem_kn_w=mem_kn_w[l], w_out=w_out[l])
        lam_init = 0.8 - 0.6 * math.exp(-0.3 * l)
        lam = (jnp.exp(jnp.sum(da_lq1[l].astype(f32) * da_lk1[l].astype(f32)))
               - jnp.exp(jnp.sum(da_lq2[l].astype(f32) * da_lk2[l].astype(f32))) + lam_init)

        hp = rms_norm(xp, norm_w[l])
        dq, dk, dv, ql, qp, ck, kp, mq, gates = project(hp, pos_p, p)
        mk, mv = mem_kv(mem_prompt, p)

        def prompt_block(qs, qpos):
            o_d = diff_attend(qs[0], dk, dv, qpos, pos_p, lam)
            o_l = mla_attend(qs[1], qs[2], ck, kp, qpos, pos_p)
            return o_d, o_l

        o_da, o_lat = causal_blocks(prompt_block, (dq, ql, qp), pos_p)
        o_mem = mem_attend(mq, mk, mv)
        xp = xp + merge(o_da, o_lat, o_mem, gates, p, lam_init)
        p_dk.append(dk.reshape(B, S, DA_HEADS, DA_V_DIM))
        p_dv.append(dv)
        p_c.append(ck)
        p_r.append(kp)
        p_mk.append(mk)
        p_mv.append(mv)

        hs = rms_norm(xs, norm_w[l])
        sdq, sdk, sdv, sql, sqp, sck, skp, smq, sgates = project(hs, pos_s, p)

        def one_seq(args):
            pt, q_d, k_d, v_d, q_l, q_p, c_new, r_new = args
            k_all = jnp.concatenate([cache_diff_k[l, pt].reshape(n_past, DA_HEADS, 2, DA_HEAD_DIM), k_d], axis=0)
            v_all = jnp.concatenate([cache_diff_v[l, pt].reshape(n_past, DA_HEADS, DA_V_DIM), v_d], axis=0)
            c_all = jnp.concatenate([cache_mla_ckv[l, pt].reshape(n_past, MLA_KV_RANK), c_new], axis=0)
            r_all = jnp.concatenate([cache_mla_kpe[l, pt].reshape(n_past, MLA_ROPE), r_new], axis=0)
            o_d = diff_attend(q_d[None], k_all[None], v_all[None], pos_s, kpos_s, lam)[0]
            o_l = mla_attend(q_l[None], q_p[None], c_all[None], r_all[None], pos_s, kpos_s)[0]
            return o_d, o_l

        so_da, so_lat = lax.map(one_seq, (page_table, sdq, sdk, sdv, sql, sqp, sck, skp))
        so_mem = mem_attend(smq, cache_mem_k[l], cache_mem_v[l])
        xs = xs + merge(so_da, so_lat, so_mem, sgates, p, lam_init)
        s_dk.append(sdk.reshape(DB, DS, DA_HEADS, DA_V_DIM))
        s_dv.append(sdv)
        s_c.append(sck)
        s_r.append(skp)

    y_prompt = xp
    y_sample = xs
    return (y_prompt, y_sample,
            jnp.stack(p_dk, 0), jnp.stack(p_dv, 0), jnp.stack(p_c, 0), jnp.stack(p_r, 0),
            jnp.stack(p_mk, 0), jnp.stack(p_mv, 0),
            jnp.stack(s_dk, 0), jnp.stack(s_dv, 0), jnp.stack(s_c, 0), jnp.stack(s_r, 0))
```

```python
import functools
import math

import jax
import jax.numpy as jnp
import numpy as np
from jax import lax
from jax.experimental import pallas as pl
from jax.experimental.pallas import tpu as pltpu

D_MODEL = 1024
PAGE_SIZE = 128
ROPE_THETA = 500000.0
EPS = 1e-6
DA_HEADS, DA_HEAD_DIM = 4, 64
DA_V_DIM = 2 * DA_HEAD_DIM
DA_WIDTH = DA_HEADS * DA_V_DIM
DA_ROT = DA_HEAD_DIM // 4
MLA_HEADS, MLA_NOPE, MLA_ROPE, MLA_V = 4, 64, 32, 64
MLA_QK = MLA_NOPE + MLA_ROPE
MLA_Q_RANK, MLA_KV_RANK = 256, 128
MLA_WIDTH = MLA_HEADS * MLA_V
MEM_HEADS, MEM_HEAD_DIM = 4, 64
MEM_WIDTH = MEM_HEADS * MEM_HEAD_DIM
MIX_WIDTH = DA_WIDTH + MLA_WIDTH + MEM_WIDTH
LAM_INIT = 0.8 - 0.6 * math.exp(-0.3 * 0)

LANES = 128
LOG2E = math.log2(math.e)

TM = 256
TQ = 256
TK = 512
PG = 8

C_DQ, C_DK, C_DV, C_DG = 0, 512, 1024, 1536
C_MQ, C_MKV, C_MKPE, C_MG = 2048, 2304, 2432, 2560
C_MEMQ, C_MEMG, C_END = 2816, 3072, 3328

BF = jnp.bfloat16
F32 = jnp.float32
NT = (((1,), (1,)), ((), ()))


def _dot(a, b):
    return jnp.dot(a, b, preferred_element_type=F32)


def _dot_nt(a, b):
    return lax.dot_general(a, b, NT, preferred_element_type=F32)


def _rsqrt_mean(x, n):
    return lax.rsqrt(jnp.sum(x * x, axis=-1, keepdims=True) * (1.0 / n) + EPS)


def _group_rms(u, g_ref, n, w_row):
    ssq = _dot((u * u).astype(BF), g_ref[...])
    return u * lax.rsqrt(ssq * (1.0 / n) + EPS) * w_row


def _rope(x, c_ref, s_ref, period, half):
    lane = lax.broadcasted_iota(jnp.int32, x.shape, 1) & (period - 1)
    partner = jnp.where(lane < half, pltpu.roll(x, LANES - half, 1), pltpu.roll(x, half, 1))
    return x * c_ref[...] + partner * s_ref[...]


def _silu(u):
    return u * (1.0 / (1.0 + jnp.exp(-u)))


def _memkv_kernel(mem_ref, nw_ref, w_ref, g_ref, kn_ref, k_ref, v_ref, kb_ref, vb_ref):
    x = mem_ref[...]
    h = (x * _rsqrt_mean(x, D_MODEL) * nw_ref[...]).astype(BF)
    k = _group_rms(_dot(h, w_ref[:, :MEM_WIDTH]), g_ref, MEM_HEAD_DIM, kn_ref[...])
    v = _dot(h, w_ref[:, MEM_WIDTH:])
    k_ref[...] = k
    v_ref[...] = v
    kb_ref[...] = k.astype(BF)
    vb_ref[...] = v.astype(BF)


def _mem_kv(mem, nw, w_bf, g256, kn_row):
    m = mem.shape[0]
    full = lambda shape: pl.BlockSpec(shape, lambda i: (0,) * len(shape))
    return pl.pallas_call(
        _memkv_kernel,
        grid=(1,),
        in_specs=[full((m, D_MODEL)), full((1, D_MODEL)), full((D_MODEL, 2 * MEM_WIDTH)),
                  full((MEM_WIDTH, MEM_WIDTH)), full((1, MEM_WIDTH))],
        out_specs=[full((m, MEM_WIDTH))] * 4,
        out_shape=[jax.ShapeDtypeStruct((m, MEM_WIDTH), F32)] * 2
        + [jax.ShapeDtypeStruct((m, MEM_WIDTH), BF)] * 2,
        name="mem_kv",
    )(mem, nw, w_bf, g256, kn_row)


def _mem_attend(memq, mk, mv):
    lane = lax.broadcasted_iota(jnp.int32, memq.shape, 1)
    out = jnp.zeros(memq.shape, F32)
    for h in range(MEM_HEADS):
        sel = (lane >= h * MEM_HEAD_DIM) & (lane < (h + 1) * MEM_HEAD_DIM)
        s = _dot_nt(jnp.where(sel, memq, 0.0).astype(BF), mk)
        p = jnp.exp2(s - jnp.max(s, axis=-1, keepdims=True))
        o = _dot(p.astype(BF), mv) * (1.0 / jnp.sum(p, axis=-1, keepdims=True))
        out = jnp.where(sel, o, out)
    return out


def _proj_kernel(with_mem, x_ref, nw_ref, win_ref, c1_ref, s1_ref, c2_ref, s2_ref,
                 wqn_ref, wkn_ref, g512_ref, wqa_ref, wuq_ref, g384_ref, wmq_ref, wcomb_ref,
                 wkv_ref, wkpe_ref, wmemq_ref, g256_ref, *rest):
    if with_mem:
        mk_ref, mv_ref = rest[:2]
        rest = rest[2:]
    (qd_ref, dk_ref, dv_ref, ckv_ref, kpe_ref, dkb_ref, dvb_ref, kmla_ref, qmla_ref,
     gates_ref, mem_ref) = rest

    x = x_ref[...]
    hb = (x * _rsqrt_mean(x, D_MODEL) * nw_ref[...]).astype(BF)
    proj = lambda lo, hi: _dot(hb, win_ref[:, lo:hi])
    lane = lax.broadcasted_iota(jnp.int32, (x.shape[0], LANES), 1)

    def rope_da(u):
        return jnp.concatenate(
            [_rope(u[:, c * LANES:(c + 1) * LANES], c1_ref, s1_ref, DA_HEAD_DIM, DA_ROT // 2)
             for c in range(DA_WIDTH // LANES)], axis=1)

    q = rope_da(_group_rms(proj(C_DQ, C_DK), g512_ref, DA_HEAD_DIM, wqn_ref[...]))
    q = q * (DA_HEAD_DIM ** -0.5 * LOG2E)
    for h in range(DA_HEADS):
        qh = q[:, h * LANES:(h + 1) * LANES]
        qd_ref[0, h, 0] = jnp.where(lane < DA_HEAD_DIM, qh, 0.0).astype(BF)
        qd_ref[0, h, 1] = jnp.where(lane >= DA_HEAD_DIM, qh, 0.0).astype(BF)
    k = rope_da(_group_rms(proj(C_DK, C_DV), g512_ref, DA_HEAD_DIM, wkn_ref[...]))
    dk_ref[...] = k
    dkb_ref[...] = k.astype(BF)
    v = proj(C_DV, C_DG)
    dv_ref[...] = v
    dvb_ref[...] = v.astype(BF)

    gates_ref[:, 0:DA_WIDTH] = _silu(proj(C_DG, C_MQ))
    gates_ref[:, DA_WIDTH:DA_WIDTH + MLA_WIDTH] = _silu(proj(C_MG, C_MEMQ))
    gates_ref[:, DA_WIDTH + MLA_WIDTH:] = _silu(proj(C_MEMG, C_END))

    mq = proj(C_MQ, C_MKV)
    mqn = (mq * _rsqrt_mean(mq, MLA_Q_RANK) * wqa_ref[...]).astype(BF)
    qm = _group_rms(_dot(mqn, wuq_ref[...]), g384_ref, MLA_QK, wmq_ref[...])
    nope_w = MLA_HEADS * MLA_NOPE
    pe = _rope(qm[:, nope_w:], c2_ref, s2_ref, MLA_ROPE, MLA_ROPE // 2)
    qcat = (jnp.concatenate([qm[:, :nope_w], pe], axis=1) * (MLA_QK ** -0.5 * LOG2E)).astype(BF)
    qabs = _dot(qcat, wcomb_ref[...]).astype(BF)
    for h in range(MLA_HEADS):
        qmla_ref[0, h] = qabs[:, h * 2 * LANES:(h + 1) * 2 * LANES]

    kv = proj(C_MKV, C_MKPE)
    ckv = kv * _rsqrt_mean(kv, MLA_KV_RANK) * wkv_ref[...]
    ckv_ref[...] = ckv
    kp = proj(C_MKPE, C_MG)
    kp = _rope(kp * _rsqrt_mean(kp, MLA_ROPE) * wkpe_ref[...], c2_ref, s2_ref, MLA_ROPE, MLA_ROPE // 2)
    kpe_ref[...] = kp[:, :MLA_ROPE]
    kmla_ref[...] = jnp.concatenate([ckv, kp], axis=1).astype(BF)

    memq = _group_rms(proj(C_MEMQ, C_MEMG), g256_ref, MEM_HEAD_DIM, wmemq_ref[...])
    memq = memq * (MEM_HEAD_DIM ** -0.5 * LOG2E)
    if with_mem:
        mem_ref[...] = _mem_attend(memq, mk_ref[...], mv_ref[...])
    else:
        mem_ref[...] = memq.astype(BF)


def _project(x, tables, consts, mem_kv_bf=None):
    t = x.shape[0]
    assert t % TM == 0
    nt = t // TM
    with_mem = mem_kv_bf is not None
    row = lambda w: pl.BlockSpec((TM, w), lambda i: (i, 0))
    full = lambda a: pl.BlockSpec(a.shape, lambda i: (0,) * a.ndim)
    ins = [x, consts["norm_w"], consts["w_in"], *tables,
           consts["wqn"], consts["wkn"], consts["g512"], consts["wqa"], consts["wuq"], consts["g384"],
           consts["wmq"], consts["wcomb"], consts["wkv"], consts["wkpe"], consts["wmemq"], consts["g256"]]
    in_specs = [row(D_MODEL), full(ins[1]), full(ins[2])] + [row(LANES)] * 4 + [full(a) for a in ins[7:]]
    if with_mem:
        ins += list(mem_kv_bf)
        in_specs += [full(a) for a in mem_kv_bf]
    out_shape = [
        jax.ShapeDtypeStruct((nt, DA_HEADS, 2, TM, LANES), BF),
        jax.ShapeDtypeStruct((t, DA_WIDTH), F32),
        jax.ShapeDtypeStruct((t, DA_WIDTH), F32),
        jax.ShapeDtypeStruct((t, MLA_KV_RANK), F32),
        jax.ShapeDtypeStruct((t, MLA_ROPE), F32),
        jax.ShapeDtypeStruct((t, DA_WIDTH), BF),
        jax.ShapeDtypeStruct((t, DA_WIDTH), BF),
        jax.ShapeDtypeStruct((t, 2 * LANES), BF),
        jax.ShapeDtypeStruct((nt, MLA_HEADS, TM, 2 * LANES), BF),
        jax.ShapeDtypeStruct((t, MIX_WIDTH), F32),
        jax.ShapeDtypeStruct((t, MEM_WIDTH), F32 if with_mem else BF),
    ]
    out_specs = [
        pl.BlockSpec((1, DA_HEADS, 2, TM, LANES), lambda i: (i, 0, 0, 0, 0)),
        row(DA_WIDTH), row(DA_WIDTH), row(MLA_KV_RANK), row(MLA_ROPE),
        row(DA_WIDTH), row(DA_WIDTH), row(2 * LANES),
        pl.BlockSpec((1, MLA_HEADS, TM, 2 * LANES), lambda i: (i, 0, 0, 0)),
        row(MIX_WIDTH), row(MEM_WIDTH),
    ]
    return pl.pallas_call(
        functools.partial(_proj_kernel, with_mem),
        grid=(nt,),
        in_specs=in_specs,
        out_specs=out_specs,
        out_shape=out_shape,
        compiler_params=pltpu.CompilerParams(dimension_semantics=("arbitrary",),
                                             vmem_limit_bytes=48 * 1024 * 1024),
        name="proj_mem" if with_mem else "proj",
    )(*ins)


def _online_update(s, v, m_ref, l_ref, acc_ref, idx):
    m_prev = m_ref[idx]
    m_new = jnp.maximum(m_prev, jnp.max(s, axis=-1, keepdims=True))
    p = jnp.exp2(s - m_new[:, :1])
    alpha = jnp.exp2(m_prev - m_new)
    l_ref[idx] = alpha * l_ref[idx] + jnp.sum(p, axis=-1, keepdims=True)
    acc_ref[idx] = alpha * acc_ref[idx] + _dot(p.astype(BF), v)
    m_ref[idx] = m_new


def _lambda(lam_ref):
    a = jnp.sum(lam_ref[0:1, :] * lam_ref[1:2, :], axis=-1, keepdims=True)
    b = jnp.sum(lam_ref[2:3, :] * lam_ref[3:4, :], axis=-1, keepdims=True)
    return jnp.exp(a) - jnp.exp(b) + LAM_INIT


def _attn_kernel(qi_ref, ki_ref, qd_ref, qm_ref, k_ref, v_ref, kmla_ref, lam_ref,
                 oda_ref, olat_ref, accd, md, ld, accm, mm, lm):
    t = pl.program_id(0)
    qi = qi_ref[t]
    ki = ki_ref[t]
    diag = (qi * TQ) // TK

    @pl.when(ki == 0)
    def _():
        md[...] = jnp.full(md.shape, -jnp.inf, F32)
        mm[...] = jnp.full(mm.shape, -jnp.inf, F32)
        for r in (ld, accd, lm, accm):
            r[...] = jnp.zeros(r.shape, F32)

    def step(masked):
        def causal(s):
            if not masked:
                return s
            qpos = qi * TQ + (lax.broadcasted_iota(jnp.int32, s.shape, 0) & (TQ - 1))
            kpos = ki * TK + lax.broadcasted_iota(jnp.int32, s.shape, 1)
            return jnp.where(kpos <= qpos, s, -jnp.inf)

        for h in range(DA_HEADS):
            sl = slice(h * LANES, (h + 1) * LANES)
            q = qd_ref[0, h].reshape(2 * TQ, LANES)
            _online_update(causal(_dot_nt(q, k_ref[:, sl])), v_ref[:, sl], md, ld, accd, h)
        kml = kmla_ref[...]
        for h in range(MLA_HEADS):
            _online_update(causal(_dot_nt(qm_ref[0, h], kml)), kml[:, :LANES], mm, lm, accm, h)

    @pl.when(ki < diag)
    def _():
        step(False)

    @pl.when(ki == diag)
    def _():
        step(True)
        lam = _lambda(lam_ref)
        for h in range(DA_HEADS):
            n = accd[h] * (1.0 / ld[h])
            oda_ref[:, h * LANES:(h + 1) * LANES] = n[:TQ] - lam * n[TQ:]
        for h in range(MLA_HEADS):
            olat_ref[:, h * LANES:(h + 1) * LANES] = accm[h] * (1.0 / lm[h])


def _prompt_attention(qd, qmla, dk_bf, dv_bf, kmla_bf, lam_rows):
    s = dk_bf.shape[0]
    assert s % TK == 0 and TK % TQ == 0
    nq = s // TQ
    qi = np.concatenate([np.full((i * TQ) // TK + 1, i, np.int32) for i in range(nq)])
    ki = np.concatenate([np.arange((i * TQ) // TK + 1, dtype=np.int32) for i in range(nq)])
    grid_spec = pltpu.PrefetchScalarGridSpec(
        num_scalar_prefetch=2,
        grid=(len(qi),),
        in_specs=[
            pl.BlockSpec((1, DA_HEADS, 2, TQ, LANES), lambda t, qi, ki: (qi[t], 0, 0, 0, 0)),
            pl.BlockSpec((1, MLA_HEADS, TQ, 2 * LANES), lambda t, qi, ki: (qi[t], 0, 0, 0)),
            pl.BlockSpec((TK, DA_WIDTH), lambda t, qi, ki: (ki[t], 0)),
            pl.BlockSpec((TK, DA_WIDTH), lambda t, qi, ki: (ki[t], 0)),
            pl.BlockSpec((TK, 2 * LANES), lambda t, qi, ki: (ki[t], 0)),
            pl.BlockSpec((4, DA_HEAD_DIM), lambda t, qi, ki: (0, 0)),
        ],
        out_specs=[pl.BlockSpec((TQ, DA_WIDTH), lambda t, qi, ki: (qi[t], 0)),
                   pl.BlockSpec((TQ, MLA_HEADS * MLA_KV_RANK), lambda t, qi, ki: (qi[t], 0))],
        scratch_shapes=[pltpu.VMEM((DA_HEADS, 2 * TQ, LANES), F32)] * 3
        + [pltpu.VMEM((MLA_HEADS, TQ, LANES), F32)] * 3,
    )
    return pl.pallas_call(
        _attn_kernel,
        grid_spec=grid_spec,
        out_shape=[jax.ShapeDtypeStruct((s, DA_WIDTH), F32),
                   jax.ShapeDtypeStruct((s, MLA_HEADS * MLA_KV_RANK), F32)],
        compiler_params=pltpu.CompilerParams(dimension_semantics=("arbitrary",),
                                             vmem_limit_bytes=48 * 1024 * 1024),
        name="prompt_attn",
    )(jnp.asarray(qi), jnp.asarray(ki), qd, qmla, dk_bf, dv_bf, kmla_bf, lam_rows)


def _decode_kernel(npg, pt_ref, *refs):
    kp = refs[0 * npg:1 * npg]
    vp = refs[1 * npg:2 * npg]
    cp = refs[2 * npg:3 * npg]
    rp = refs[3 * npg:4 * npg]
    (qd_ref, qm_ref, kn_ref, vn_ref, kmn_ref, memq_ref, mk_ref, mv_ref, lam_ref,
     oda_ref, olat_ref, omem_ref, accd, md, ld, accm, mm, lm, kpe_buf) = refs[4 * npg:]
    c = pl.program_id(1)

    @pl.when(c == 0)
    def _():
        md[...] = jnp.full(md.shape, -jnp.inf, F32)
        mm[...] = jnp.full(mm.shape, -jnp.inf, F32)
        for r in (ld, accd, lm, accm, kpe_buf):
            r[...] = jnp.zeros(r.shape, r.dtype)

    def attend(k, v, kmla, new):
        def causal(s):
            if not new:
                return s
            qtok = lax.broadcasted_iota(jnp.int32, s.shape, 0) & 3
            ktok = lax.broadcasted_iota(jnp.int32, s.shape, 1)
            return jnp.where(ktok <= qtok, s, -jnp.inf)

        for h in range(DA_HEADS):
            sl = slice(h * LANES, (h + 1) * LANES)
            _online_update(causal(_dot_nt(qd_ref[0, h], k[:, sl])), v[:, sl], md, ld, accd, h)
        _online_update(causal(_dot_nt(qm_ref[0], kmla)), kmla[:, :LANES], mm, lm, accm, 0)

    for p in range(npg):
        kpe_buf[p * PAGE_SIZE:(p + 1) * PAGE_SIZE, 0:MLA_ROPE] = rp[p][0]
    cat = lambda rs: jnp.concatenate([r[0] for r in rs], axis=0)
    kmla = jnp.concatenate([cat(cp), kpe_buf[...]], axis=1).astype(BF)
    attend(cat(kp).astype(BF), cat(vp).astype(BF), kmla, False)

    @pl.when(c == pl.num_programs(1) - 1)
    def _():
        pad = lambda r: jnp.concatenate(
            [r[0], jnp.zeros((PAGE_SIZE - r.shape[1], r.shape[2]), r.dtype)], axis=0).astype(BF)
        attend(pad(kn_ref), pad(vn_ref), pad(kmn_ref), True)
        lam = _lambda(lam_ref)
        for h in range(DA_HEADS):
            n = accd[h] * (1.0 / ld[h])
            oda_ref[0, :, h * LANES:(h + 1) * LANES] = n - lam * pltpu.roll(n, 4, 0)
        olat_ref[0] = accm[0] * (1.0 / lm[0])
        omem_ref[0] = _mem_attend(memq_ref[0].astype(F32), mk_ref[0].astype(BF), mv_ref[0].astype(BF))


def _decode_attention(page_table, ck, cv, cc, cr, qd_s, qm_s, kn, vn, kmn, memq, cmk, cmv, lam_rows):
    db, n_pages = page_table.shape
    npg = min(PG, n_pages)
    assert n_pages % npg == 0
    n_mem = cmk.shape[1]
    rows_m = MLA_HEADS * 4

    def page(width):
        return [pl.BlockSpec((1, PAGE_SIZE, width), (lambda b, c, pt, p=p: (pt[b, c * npg + p], 0, 0)))
                for p in range(npg)]

    per_seq = lambda *shape: pl.BlockSpec((1,) + shape, lambda b, c, pt: (b,) + (0,) * len(shape))
    grid_spec = pltpu.PrefetchScalarGridSpec(
        num_scalar_prefetch=1,
        grid=(db, n_pages // npg),
        in_specs=page(DA_WIDTH) + page(DA_WIDTH) + page(MLA_KV_RANK) + page(MLA_ROPE) + [
            per_seq(DA_HEADS, 8, LANES), per_seq(rows_m, 2 * LANES),
            per_seq(8, DA_WIDTH), per_seq(8, DA_WIDTH), per_seq(8, 2 * LANES),
            per_seq(8, MEM_WIDTH), per_seq(n_mem, MEM_WIDTH), per_seq(n_mem, MEM_WIDTH),
            pl.BlockSpec((4, DA_HEAD_DIM), lambda b, c, pt: (0, 0))],
        out_specs=[per_seq(8, DA_WIDTH), per_seq(rows_m, MLA_KV_RANK), per_seq(8, MEM_WIDTH)],
        scratch_shapes=[pltpu.VMEM((DA_HEADS, 8, LANES), F32)] * 3 + [pltpu.VMEM((1, rows_m, LANES), F32)] * 3
        + [pltpu.VMEM((npg * PAGE_SIZE, LANES), F32)],
    )
    return pl.pallas_call(
        functools.partial(_decode_kernel, npg),
        grid_spec=grid_spec,
        out_shape=[jax.ShapeDtypeStruct((db, 8, DA_WIDTH), F32),
                   jax.ShapeDtypeStruct((db, rows_m, MLA_KV_RANK), F32),
                   jax.ShapeDtypeStruct((db, 8, MEM_WIDTH), F32)],
        compiler_params=pltpu.CompilerParams(dimension_semantics=("arbitrary", "arbitrary"),
                                             vmem_limit_bytes=48 * 1024 * 1024),
        name="decode_attn",
    )(page_table, *([ck] * npg), *([cv] * npg), *([cc] * npg), *([cr] * npg),
      qd_s, qm_s, kn, vn, kmn, memq, cmk, cmv, lam_rows)


def _merge_kernel(x_ref, oda_ref, olat_ref, omem_ref, gates_ref, subln_ref, wuv_ref, wout_ref, y_ref):
    parts = []
    for h in range(DA_HEADS):
        o = oda_ref[:, h * LANES:(h + 1) * LANES]
        parts.append(o * _rsqrt_mean(o, DA_V_DIM) * subln_ref[...] * (1.0 - LAM_INIT))
    parts.append(_dot(olat_ref[...].astype(BF), wuv_ref[...]))
    parts.append(omem_ref[...])
    o = (jnp.concatenate(parts, axis=1) * gates_ref[...]).astype(BF)
    y_ref[...] = x_ref[...] + _dot(o, wout_ref[...])


def _merge(x, oda, olat, omem, gates, consts):
    t = x.shape[0]
    assert t % TM == 0
    row = lambda w: pl.BlockSpec((TM, w), lambda i: (i, 0))
    full = lambda a: pl.BlockSpec(a.shape, lambda i: (0,) * a.ndim)
    ws = [consts["subln"], consts["wuv"], consts["w_out"]]
    return pl.pallas_call(
        _merge_kernel,
        grid=(t // TM,),
        in_specs=[row(D_MODEL), row(DA_WIDTH), row(MLA_HEADS * MLA_KV_RANK), row(MEM_WIDTH), row(MIX_WIDTH)]
        + [full(a) for a in ws],
        out_specs=row(D_MODEL),
        out_shape=jax.ShapeDtypeStruct((t, D_MODEL), F32),
        compiler_params=pltpu.CompilerParams(dimension_semantics=("arbitrary",)),
        name="merge",
    )(x, oda, olat, omem, gates, *ws)


def _group_matrix(group_ids):
    g = np.asarray(group_ids)
    return jnp.asarray((g[:, None] == g[None, :]).astype(np.float32), dtype=BF)


def _rope_tables(pos, period, half):
    inv_freq = ROPE_THETA ** (-jnp.arange(half, dtype=F32) / half)
    ang = pos.astype(F32)[:, None] * inv_freq[None, :]
    cos, sin = jnp.cos(ang), jnp.sin(ang)
    t = pos.shape[0]
    tail = period - 2 * half
    c = jnp.concatenate([cos, cos, jnp.ones((t, tail), F32)], axis=1)
    s = jnp.concatenate([-sin, sin, jnp.zeros((t, tail), F32)], axis=1)
    reps = LANES // period
    return jnp.tile(c, (1, reps)), jnp.tile(s, (1, reps))


def _prepare_consts(norm_w, w_in, da_qn_w, da_kn_w, da_subln_w, mla_qa_norm_w, w_mla_uq, mla_qn_w,
                    w_mla_uk, mla_kv_norm_w, mla_kn_w, w_mla_uv, mem_qn_w, w_out):
    w = w_in
    splits = np.cumsum([0, DA_WIDTH, DA_WIDTH, DA_WIDTH, DA_WIDTH, MLA_Q_RANK, MLA_KV_RANK, MLA_ROPE,
                        MLA_WIDTH, MEM_WIDTH, MEM_WIDTH])
    sec = [w[:, splits[i]:splits[i + 1]] for i in range(10)]
    sec[6] = jnp.pad(sec[6], ((0, 0), (0, LANES - MLA_ROPE)))
    w_in_p = jnp.concatenate(sec, axis=1).astype(BF)
    assert w_in_p.shape[1] == C_END

    uq = w_mla_uq.reshape(MLA_Q_RANK, MLA_HEADS, MLA_QK)
    wuq = jnp.concatenate([uq[:, :, :MLA_NOPE].reshape(MLA_Q_RANK, -1),
                           uq[:, :, MLA_NOPE:].reshape(MLA_Q_RANK, -1)], axis=1).astype(BF)
    wmq = jnp.concatenate([jnp.tile(mla_qn_w[:MLA_NOPE], MLA_HEADS),
                           jnp.tile(mla_qn_w[MLA_NOPE:], MLA_HEADS)])[None, :]
    head_of = np.concatenate([np.repeat(np.arange(MLA_HEADS), MLA_NOPE), np.repeat(np.arange(MLA_HEADS), MLA_ROPE)])
    wcomb = jnp.zeros((MLA_HEADS * MLA_QK, MLA_HEADS * 2 * LANES), F32)
    eye = jnp.eye(MLA_ROPE, dtype=F32)
    for h in range(MLA_HEADS):
        wcomb = wcomb.at[h * MLA_NOPE:(h + 1) * MLA_NOPE, h * 2 * LANES:h * 2 * LANES + MLA_KV_RANK].set(w_mla_uk[h])
        r0 = MLA_HEADS * MLA_NOPE + h * MLA_ROPE
        wcomb = wcomb.at[r0:r0 + MLA_ROPE, h * 2 * LANES + LANES:h * 2 * LANES + LANES + MLA_ROPE].set(eye)
    wuv = jnp.zeros((MLA_HEADS * MLA_KV_RANK, MLA_WIDTH), F32)
    for h in range(MLA_HEADS):
        wuv = wuv.at[h * MLA_KV_RANK:(h + 1) * MLA_KV_RANK, h * MLA_V:(h + 1) * MLA_V].set(w_mla_uv[h])
    return dict(
        norm_w=norm_w[None, :], w_in=w_in_p,
        wqn=jnp.tile(da_qn_w, 2 * DA_HEADS)[None, :], wkn=jnp.tile(da_kn_w, 2 * DA_HEADS)[None, :],
        g512=_group_matrix(np.arange(DA_WIDTH) // DA_HEAD_DIM),
        wqa=mla_qa_norm_w[None, :], wuq=wuq, g384=_group_matrix(head_of), wmq=wmq, wcomb=wcomb.astype(BF),
        wkv=mla_kv_norm_w[None, :], wkpe=jnp.pad(mla_kn_w, (0, LANES - MLA_ROPE))[None, :],
        wmemq=jnp.tile(mem_qn_w, MEM_HEADS)[None, :], g256=_group_matrix(np.arange(MEM_WIDTH) // MEM_HEAD_DIM),
        subln=da_subln_w[None, :], wuv=wuv.astype(BF), w_out=w_out.astype(BF),
    )


def kernel(x_prompt, x_sample, mem_prompt, cache_diff_k, cache_diff_v, cache_mla_ckv, cache_mla_kpe,
           cache_mem_k, cache_mem_v, page_table, norm_w, w_in, da_qn_w, da_kn_w, da_lq1, da_lk1,
           da_lq2, da_lk2, da_subln_w, mla_qa_norm_w, w_mla_uq, mla_qn_w, w_mla_uk, mla_kv_norm_w,
           mla_kn_w, w_mla_uv, mem_norm_w, w_mem_kv, mem_qn_w, mem_kn_w, w_out):
    assert x_prompt.shape[0] == 1 and norm_w.shape[0] == 1, "single layer, single prompt"
    s = x_prompt.shape[1]
    db, ds = x_sample.shape[:2]
    assert ds == 4
    n_pages = page_table.shape[1]
    n_phys = cache_diff_k.shape[1]
    past_len = n_pages * PAGE_SIZE

    consts = _prepare_consts(norm_w[0], w_in[0], da_qn_w[0], da_kn_w[0], da_subln_w[0], mla_qa_norm_w[0],
                             w_mla_uq[0], mla_qn_w[0], w_mla_uk[0], mla_kv_norm_w[0], mla_kn_w[0],
                             w_mla_uv[0], mem_qn_w[0], w_out[0])
    lam_rows = jnp.stack([da_lq1[0], da_lk1[0], da_lq2[0], da_lk2[0]])
    g256 = consts["g256"]

    mk, mv, mk_bf, mv_bf = _mem_kv(mem_prompt[0], mem_norm_w, w_mem_kv[0].astype(BF), g256,
                                   jnp.tile(mem_kn_w[0], MEM_HEADS)[None, :])
    pos_p = jnp.arange(s, dtype=jnp.int32)
    tab_p = _rope_tables(pos_p, DA_HEAD_DIM, DA_ROT // 2) + _rope_tables(pos_p, MLA_ROPE, MLA_ROPE // 2)
    xp = x_prompt[0]
    (qd, dk, dv, ckv, kpe, dk_bf, dv_bf, kmla_bf, qmla, gates, omem) = _project(xp, tab_p, consts, (mk_bf, mv_bf))
    oda, olat = _prompt_attention(qd, qmla, dk_bf, dv_bf, kmla_bf, lam_rows)
    y_prompt = _merge(xp, oda, olat, omem, gates, consts)

    t_s = db * ds
    pad_t = (-t_s) % TM
    xs = jnp.pad(x_sample.reshape(t_s, D_MODEL), ((0, pad_t), (0, 0)))
    pos_s = jnp.pad(past_len + jnp.tile(jnp.arange(ds, dtype=jnp.int32), db), (0, pad_t))
    tab_s = _rope_tables(pos_s, DA_HEAD_DIM, DA_ROT // 2) + _rope_tables(pos_s, MLA_ROPE, MLA_ROPE // 2)
    (sqd, sdk, sdv, sckv, skpe, _, _, skmla_bf, sqmla, sgates, smemq) = _project(xs, tab_s, consts)
    nts = xs.shape[0] // TM
    sqd = sqd.transpose(1, 2, 0, 3, 4).reshape(DA_HEADS, 2, nts * TM, LANES)[:, :, :t_s]
    sqd = sqd.reshape(DA_HEADS, 2, db, ds, LANES).transpose(2, 0, 1, 3, 4).reshape(db, DA_HEADS, 2 * ds, LANES)
    sqmla = sqmla.transpose(1, 0, 2, 3).reshape(MLA_HEADS, nts * TM, 2 * LANES)[:, :t_s]
    sqmla = sqmla.reshape(MLA_HEADS, db, ds, 2 * LANES).transpose(1, 0, 2, 3).reshape(db, MLA_HEADS * ds, 2 * LANES)
    pad8 = lambda a: jnp.pad(a[:t_s].reshape(db, ds, a.shape[-1]), ((0, 0), (0, 8 - ds), (0, 0)))
    soda, solat, somem = _decode_attention(
        page_table,
        cache_diff_k[0].reshape(n_phys, PAGE_SIZE, DA_WIDTH), cache_diff_v[0].reshape(n_phys, PAGE_SIZE, DA_WIDTH),
        cache_mla_ckv[0], cache_mla_kpe[0],
        sqd, sqmla, pad8(sdk), pad8(sdv), pad8(skmla_bf.astype(F32)), pad8(smemq),
        cache_mem_k[0].reshape(db, -1, MEM_WIDTH), cache_mem_v[0].reshape(db, -1, MEM_WIDTH), lam_rows)
    unpad = lambda a: jnp.pad(a[:, :ds].reshape(t_s, a.shape[-1]), ((0, pad_t), (0, 0)))
    solat = solat.reshape(db, MLA_HEADS, ds, MLA_KV_RANK).transpose(0, 2, 1, 3).reshape(db, ds, -1)
    y_sample = _merge(xs, unpad(soda), unpad(solat), unpad(somem), sgates, consts)[:t_s]

    return (y_prompt[None], y_sample.reshape(db, ds, D_MODEL),
            dk.reshape(1, 1, s, DA_HEADS, DA_V_DIM), dv.reshape(1, 1, s, DA_HEADS, DA_V_DIM),
            ckv[None, None], kpe[None, None],
            mk.reshape(1, 1, -1, MEM_HEADS, MEM_HEAD_DIM), mv.reshape(1, 1, -1, MEM_HEADS, MEM_HEAD_DIM),
            sdk[:t_s].reshape(1, db, ds, DA_HEADS, DA_V_DIM), sdv[:t_s].reshape(1, db, ds, DA_HEADS, DA_V_DIM),
            sckv[:t_s].reshape(1, db, ds, MLA_KV_RANK), skpe[:t_s].reshape(1, db, ds, MLA_ROPE))
```
